```python
import jax, jax.numpy as jnp
from jax import lax
import numpy as np

D_MODEL = 1024
BATCH = 2
SEQ = 16384
DEPTH = 4

N_MIXERS = 3
D_HEAD = 64
ROT_DIM = D_HEAD // 4
ROPE_THETA = 500000.0
D_FF = 2816
PLE_DIM = 256
RMS_EPS = 1e-6
NEG_INF = -1e30
N_NORMS = 8
A_PAIRS = ((128, 1), (512, 4), (2048, 16))
A_HEADS = 8
A_BLOCK = 64
B_HEADS = 16
B_KV_HEADS = 4
B_RADIUS = 128
B_BLOCK = 128
C_HEADS = 16
GRID_W = 64
NA_ROWS = 8
NA_COLS = 16
NA_QC = 16
NA_KC = 2 * NA_QC

kernel_name = "hybrid_dilated_swa_natten_macaron"


def rms_norm(x, g):
    x32 = x.astype(jnp.float32)
    y = x32 * lax.rsqrt(jnp.mean(x32 * x32, axis=-1, keepdims=True) + RMS_EPS)
    return (y * g.astype(jnp.float32)).astype(x.dtype)


def swiglu(x, wi, wo):
    gate, up = jnp.split(x @ wi, 2, axis=-1)
    return (jax.nn.silu(gate) * up) @ wo


def rope_tables(seq):
    pos = jnp.arange(seq, dtype=jnp.float32)
    inv = ROPE_THETA ** (-jnp.arange(0, ROT_DIM, 2, dtype=jnp.float32) / ROT_DIM)
    ang = pos[:, None] * inv[None, :]
    return jnp.cos(ang), jnp.sin(ang)


def apply_rope(x, cos, sin):
    shape = (cos.shape[0],) + (1,) * (x.ndim - 3) + (cos.shape[1],)
    c = cos.reshape(shape).astype(x.dtype)
    s = sin.reshape(shape).astype(x.dtype)
    half = ROT_DIM // 2
    x1, x2, rest = x[..., :half], x[..., half:ROT_DIM], x[..., ROT_DIM:]
    return jnp.concatenate([x1 * c - x2 * s, x2 * c + x1 * s, rest], axis=-1)


def band_attention(q, k, v, radius, block, sink=None):
    n, length, hk, grp, dh = q.shape
    nb = -(-length // block)
    pad = nb * block - length
    qb = jnp.pad(q, ((0, 0), (0, pad), (0, 0), (0, 0), (0, 0))).reshape(n, nb, block, hk, grp, dh)

    def three_blocks(a):
        a = jnp.pad(a, ((0, 0), (block, pad + block), (0, 0), (0, 0))).reshape(n, nb + 2, block, hk, dh)
        return jnp.concatenate([a[:, :-2], a[:, 1:-1], a[:, 2:]], axis=2)

    kb, vb = three_blocks(k), three_blocks(v)
    qpos = jnp.arange(nb * block).reshape(nb, block)
    kpos = jnp.arange(nb)[:, None] * block - block + jnp.arange(3 * block)[None, :]
    rel = kpos[:, None, :] - qpos[:, :, None]
    valid = (jnp.abs(rel) <= radius) & (kpos[:, None, :] >= 0) & (kpos[:, None, :] < length)
    s = jnp.einsum('nbqhgd,nbkhd->nbhgqk', qb, kb).astype(jnp.float32) * (dh ** -0.5)
    s = jnp.where(valid[None, :, None, None], s, NEG_INF)
    m = jnp.max(s, axis=-1)
    if sink is not None:
        sk = sink.astype(jnp.float32)[None, None, :, :, None]
        m = jnp.maximum(m, sk)
    pr = jnp.exp(s - m[..., None])
    den = jnp.sum(pr, axis=-1)
    if sink is not None:
        den = den + jnp.exp(sk - m)
    o = jnp.einsum('nbhgqk,nbkhd->nbqhgd', pr.astype(v.dtype), vb)
    o = o / jnp.transpose(den, (0, 1, 4, 2, 3))[..., None]
    lse = jnp.transpose(m + jnp.log(den), (0, 1, 4, 2, 3))
    o = o.reshape(n, nb * block, hk, grp, dh)[:, :length].astype(q.dtype)
    lse = lse.reshape(n, nb * block, hk, grp)[:, :length]
    return o, lse


def dilated_mixer(h, wqkv, wo, cos, sin):
    b, s, _ = h.shape
    ng = len(A_PAIRS)
    qkv = (h @ wqkv).reshape(b, s, 3, ng, A_HEADS, D_HEAD)
    q = apply_rope(qkv[:, :, 0], cos, sin)
    k = apply_rope(qkv[:, :, 1], cos, sin)
    v = qkv[:, :, 2]
    outs, lses = [], []
    for g, (window, dil) in enumerate(A_PAIRS):
        sub = s // dil

        def to_sub(a):
            return a.reshape(b, sub, dil, A_HEADS, D_HEAD).transpose(0, 2, 1, 3, 4).reshape(b * dil, sub, A_HEADS, D_HEAD)

        o, lse = band_attention(to_sub(q[:, :, g])[:, :, :, None], to_sub(k[:, :, g]), to_sub(v[:, :, g]),
                                window // (2 * dil), A_BLOCK)
        o = o[:, :, :, 0].reshape(b, dil, sub, A_HEADS, D_HEAD).transpose(0, 2, 1, 3, 4).reshape(b, s, A_HEADS, D_HEAD)
        lse = lse[..., 0].reshape(b, dil, sub, A_HEADS).transpose(0, 2, 1, 3).reshape(b, s, A_HEADS)
        outs.append(o)
        lses.append(lse)
    wts = jax.nn.softmax(jnp.stack(lses, axis=0), axis=0)
    o = jnp.sum(wts[..., None] * jnp.stack(outs, axis=0).astype(jnp.float32), axis=0).astype(h.dtype)
    return o.reshape(b, s, A_HEADS * D_HEAD) @ wo


def window_gqa_mixer(h, wqkv, wo, sink, cos, sin):
    b, s, _ = h.shape
    grp = B_HEADS // B_KV_HEADS
    qkv = h @ wqkv
    q = qkv[..., :B_HEADS * D_HEAD].reshape(b, s, B_HEADS, D_HEAD)
    k = qkv[..., B_HEADS * D_HEAD:(B_HEADS + B_KV_HEADS) * D_HEAD].reshape(b, s, B_KV_HEADS, D_HEAD)
    v = qkv[..., (B_HEADS + B_KV_HEADS) * D_HEAD:].reshape(b, s, B_KV_HEADS, D_HEAD)
    q = apply_rope(q, cos, sin).reshape(b, s, B_KV_HEADS, grp, D_HEAD)
    k = apply_rope(k, cos, sin)
    o, _ = band_attention(q, k, v, B_RADIUS, B_BLOCK, sink.reshape(B_KV_HEADS, grp))
    return o.reshape(b, s, B_HEADS * D_HEAD) @ wo


def neighbourhood_mixer(h, wqkv, wo, rpb):
    b, s, _ = h.shape
    rows = s // GRID_W
    kh = min(NA_ROWS, rows)
    ncb = GRID_W // NA_QC
    qkv = (h @ wqkv).reshape(b, rows, GRID_W, 3, C_HEADS, D_HEAD)
    q = qkv[:, :, :, 0] * (D_HEAD ** -0.5)
    k, v = qkv[:, :, :, 1], qkv[:, :, :, 2]
    qcol = np.arange(GRID_W).reshape(ncb, NA_QC)
    kstart = np.clip(np.arange(ncb) * NA_QC - NA_COLS // 2, 0, GRID_W - NA_KC)
    kcol = kstart[:, None] + np.arange(NA_KC)[None, :]
    cstart = np.clip(qcol - NA_COLS // 2, 0, GRID_W - NA_COLS)
    col_valid = (kcol[:, None, :] >= cstart[..., None]) & (kcol[:, None, :] < cstart[..., None] + NA_COLS)
    dc_idx = np.clip(kcol[:, None, :] - qcol[:, :, None] + NA_COLS - 1, 0, 2 * NA_COLS - 2)
    kc = k[:, :, kcol]
    vc = v[:, :, kcol]

    def row_fn(r):
        rs = jnp.clip(r - kh // 2, 0, rows - kh)
        kw = lax.dynamic_slice_in_dim(kc, rs, kh, axis=1)
        vw = lax.dynamic_slice_in_dim(vc, rs, kh, axis=1)
        qr = lax.dynamic_index_in_dim(q, r, axis=1, keepdims=False).reshape(b, ncb, NA_QC, C_HEADS, D_HEAD)
        sc = jnp.einsum('bnqhd,bjnchd->bhnqjc', qr, kw).astype(jnp.float32)
        dr_idx = rs + jnp.arange(kh) - r + NA_ROWS - 1
        bias = rpb[:, dr_idx][:, :, dc_idx].transpose(0, 2, 3, 1, 4)
        sc = sc + bias.astype(jnp.float32)[None]
        sc = jnp.where(col_valid[None, None, :, :, None, :], sc, NEG_INF)
        pr = jax.nn.softmax(sc.reshape(b, C_HEADS, ncb, NA_QC, kh * NA_KC), axis=-1)
        pr = pr.reshape(b, C_HEADS, ncb, NA_QC, kh, NA_KC).astype(vw.dtype)
        o = jnp.einsum('bhnqjc,bjnchd->bnqhd', pr, vw)
        return o.reshape(b, GRID_W, C_HEADS * D_HEAD)

    out = lax.map(row_fn, jnp.arange(rows))
    return out.transpose(1, 0, 2, 3).reshape(b, s, C_HEADS * D_HEAD) @ wo


def setup_inputs(seed: int = 0) -> dict:
    key = jax.random.key(seed)
    ks = jax.random.split(key, 16)
    f32 = jnp.float32

    def dense(k, shape, fan_in):
        return jax.random.normal(k, shape, f32) * (fan_in ** -0.5)

    n_a = len(range(0, DEPTH, N_MIXERS))
    n_b = len(range(1, DEPTH, N_MIXERS))
    n_c = len(range(2, DEPTH, N_MIXERS))
    ng = len(A_PAIRS)
    return {
        "x": jax.random.normal(ks[0], (BATCH, SEQ, D_MODEL), f32),
        "p": jax.random.normal(ks[1], (DEPTH, BATCH, SEQ, PLE_DIM), f32),
        "norm_g": 1.0 + 0.02 * jax.random.normal(ks[2], (DEPTH, N_NORMS, D_MODEL), f32),
        "ffn_wi": dense(ks[3], (DEPTH, 2, D_MODEL, 2 * D_FF), D_MODEL),
        "ffn_wo": dense(ks[4], (DEPTH, 2, D_FF, D_MODEL), D_FF),
        "ple_proj": dense(ks[5], (DEPTH, PLE_DIM, D_MODEL), PLE_DIM),
        "ple_gate": dense(ks[6], (DEPTH, D_MODEL, D_MODEL), D_MODEL),
        "a_wqkv": dense(ks[7], (n_a, D_MODEL, 3 * ng * A_HEADS * D_HEAD), D_MODEL),
        "a_wo": dense(ks[8], (n_a, A_HEADS * D_HEAD, D_MODEL), A_HEADS * D_HEAD),
        "b_wqkv": dense(ks[9], (n_b, D_MODEL, (B_HEADS + 2 * B_KV_HEADS) * D_HEAD), D_MODEL),
        "b_wo": dense(ks[10], (n_b, B_HEADS * D_HEAD, D_MODEL), B_HEADS * D_HEAD),
        "b_sink": 0.5 * jax.random.normal(ks[11], (n_b, B_HEADS), f32),
        "c_wqkv": dense(ks[12], (n_c, D_MODEL, 3 * C_HEADS * D_HEAD), D_MODEL),
        "c_wo": dense(ks[13], (n_c, C_HEADS * D_HEAD, D_MODEL), C_HEADS * D_HEAD),
        "c_rpb": 0.1 * jax.random.normal(ks[14], (n_c, C_HEADS, 2 * NA_ROWS - 1, 2 * NA_COLS - 1), f32),
    }


def reference(x, p, norm_g, ffn_wi, ffn_wo, ple_proj, ple_gate, a_wqkv, a_wo,
              b_wqkv, b_wo, b_sink, c_wqkv, c_wo, c_rpb):
    cos, sin = rope_tables(x.shape[1])
    h = x
    for i in range(DEPTH):
        g = norm_g[i]
        h = h + 0.5 * rms_norm(swiglu(rms_norm(h, g[0]), ffn_wi[i, 0], ffn_wo[i, 0]), g[1])
        hn = rms_norm(h, g[2])
        mixer, j = i % N_MIXERS, i // N_MIXERS
        if mixer == 0:
            y = dilated_mixer(hn, a_wqkv[j], a_wo[j], cos, sin)
        elif mixer == 1:
            y = window_gqa_mixer(hn, b_wqkv[j], b_wo[j], b_sink[j], cos, sin)
        else:
            y = neighbourhood_mixer(hn, c_wqkv[j], c_wo[j], c_rpb[j])
        h = h + rms_norm(y, g[3])
        h = h + 0.5 * rms_norm(swiglu(rms_norm(h, g[4]), ffn_wi[i, 1], ffn_wo[i, 1]), g[5])
        e = p[i].astype(h.dtype) @ ple_proj[i]
        gate = jax.nn.sigmoid(rms_norm(h, g[6]) @ ple_gate[i])
        h = h + rms_norm(e * gate, g[7])
    return h
```

```python
import functools

import numpy as np
import jax
import jax.numpy as jnp
from jax import lax
from jax.experimental import pallas as pl
from jax.experimental.pallas import tpu as pltpu

F32 = jnp.float32
BF16 = jnp.bfloat16

D_HEAD = 64
ROT_DIM = D_HEAD // 4
ROPE_THETA = 500000.0
RMS_EPS = 1e-6
NEG_INF = -1e30
N_MIXERS = 3
A_PAIRS = ((128, 1), (512, 4), (2048, 16))
A_HEADS = 8
B_HEADS = 16
B_KV_HEADS = 4
B_RADIUS = 128
C_HEADS = 16
GRID_W = 64
NA_ROWS = 8
NA_COLS = 16

LANES = 128
VMEM_LIMIT_BYTES = 56 * 1024 * 1024


def _cparams(n_axes):
    return pltpu.CompilerParams(dimension_semantics=("arbitrary",) * n_axes,
                                vmem_limit_bytes=VMEM_LIMIT_BYTES)


def _resident(shape):
    zeros = (0,) * len(shape)
    return pl.BlockSpec(shape, lambda *_: zeros, pipeline_mode=pl.Buffered(1))


def _rms(x, g):
    ms = jnp.mean(x * x, axis=-1, keepdims=True)
    return x * lax.rsqrt(ms + RMS_EPS) * g


def _dot(a, b):
    return jnp.dot(a, b, preferred_element_type=F32)


def _dot_nt(a, b):
    return lax.dot_general(a, b, (((1,), (1,)), ((), ())), preferred_element_type=F32)


def _ffn_kernel(h_ref, g_ref, wi_ref, wo_ref, o_ref, *, g_row, d_ff, chunk):
    h = h_ref[...]
    xn = _rms(h, g_ref[g_row:g_row + 1, :]).astype(BF16)
    acc = jnp.zeros(h.shape, F32)
    for c0 in range(0, d_ff, chunk):
        gate = _dot(xn, wi_ref[:, c0:c0 + chunk])
        up = _dot(xn, wi_ref[:, d_ff + c0:d_ff + c0 + chunk])
        act = (gate * jax.nn.sigmoid(gate)) * up
        acc = acc + _dot(act.astype(BF16), wo_ref[c0:c0 + chunk, :])
    o_ref[...] = h + 0.5 * _rms(acc, g_ref[g_row + 1:g_row + 2, :])


def _ffn_call(h, g8, wi, wo, *, g_row, tm):
    rows, d = h.shape
    d_ff = wo.shape[0]
    return pl.pallas_call(
        functools.partial(_ffn_kernel, g_row=g_row, d_ff=d_ff, chunk=256),
        grid=(rows // tm,),
        in_specs=[pl.BlockSpec((tm, d), lambda i: (i, 0)),
                  _resident(g8.shape), _resident(wi.shape), _resident(wo.shape)],
        out_specs=pl.BlockSpec((tm, d), lambda i: (i, 0)),
        out_shape=jax.ShapeDtypeStruct(h.shape, F32),
        compiler_params=_cparams(1),
        name="ffn",
    )(h, g8, wi, wo)


def _proj_kernel(*refs, g_row, n_rope, n_scaled, scale, chunk):
    if n_rope:
        h_ref, g_ref, w_ref, cf_ref, sa_ref, sb_ref, o_ref = refs
        cf, sa, sb = cf_ref[...], sa_ref[...], sb_ref[...]
    else:
        h_ref, g_ref, w_ref, o_ref = refs
    xn = _rms(h_ref[...], g_ref[g_row:g_row + 1, :]).astype(BF16)
    n = w_ref.shape[1]
    for c0 in range(0, n, chunk):
        y = _dot(xn, w_ref[:, c0:c0 + chunk])
        for j0 in range(0, chunk, LANES):
            yc = y[:, j0:j0 + LANES]
            if c0 + j0 < n_rope:
                yc = yc * cf + pltpu.roll(yc, LANES - ROT_DIM // 2, 1) * sa + pltpu.roll(yc, ROT_DIM // 2, 1) * sb
            if c0 + j0 < n_scaled:
                yc = yc * scale
            o_ref[:, c0 + j0:c0 + j0 + LANES] = yc.astype(BF16)


def _proj_call(h, g8, w, tables, *, seq, dil, g_row, n_rope, n_scaled, scale, tm):
    rows, d = h.shape
    n = w.shape[1]
    rsub = rows // dil
    lsub = seq // dil
    tm = min(tm, lsub)
    hv = h.reshape(rsub, dil * d)
    in_specs = [pl.BlockSpec((tm, d), lambda r, i: (i, r)), _resident(g8.shape), _resident(w.shape)]
    args = [hv, g8, w]
    if n_rope:
        tiles_per_seq = lsub // tm
        for t in tables:
            args.append(t.reshape(lsub, dil * LANES))
            in_specs.append(pl.BlockSpec((tm, LANES), lambda r, i: (i % tiles_per_seq, r)))
    out = pl.pallas_call(
        functools.partial(_proj_kernel, g_row=g_row, n_rope=n_rope, n_scaled=n_scaled, scale=scale, chunk=256),
        grid=(dil, rsub // tm),
        in_specs=in_specs,
        out_specs=pl.BlockSpec((None, tm, n), lambda r, i: (r, i, 0)),
        out_shape=jax.ShapeDtypeStruct((dil, rsub, n), BF16),
        compiler_params=_cparams(2),
        name="qkv_proj",
    )(*args)
    return out.reshape(rows, n)


def _swap_halves(x):
    half = LANES // 2
    return jnp.concatenate([x[:, half:], x[:, :half]], axis=1)


def _band_attn_kernel(*refs, radius, ch, tq, n_pairs, group, chunks_per_seq, has_sink, want_lse):
    it = iter(refs)
    q_ref, k_ref, kp_ref, kn_ref, v_ref, vp_ref, vn_ref = [next(it) for _ in range(7)]
    sink_ref = next(it) if has_sink else None
    o_ref = next(it)
    lse_ref = next(it) if want_lse else None
    kbuf, vbuf = next(it), next(it)

    w = tq + 2 * radius
    nsb = ch // tq
    kbuf[0:radius, :] = kp_ref[...]
    kbuf[radius:radius + ch, :] = k_ref[...]
    kbuf[radius + ch:radius + ch + radius, :] = kn_ref[...]
    vbuf[0:radius, :] = vp_ref[...]
    vbuf[radius:radius + ch, :] = v_ref[...]
    vbuf[radius + ch:radius + ch + radius, :] = vn_ref[...]

    cidx = pl.program_id(0) % chunks_per_seq
    seq_first = cidx == 0
    seq_last = cidx == chunks_per_seq - 1

    qi = lax.broadcasted_iota(jnp.int32, (tq, w), 0)
    ki = lax.broadcasted_iota(jnp.int32, (tq, w), 1)
    rel = ki - qi
    band_bias = jnp.where((rel >= 0) & (rel <= 2 * radius), 0.0, NEG_INF).astype(F32)
    krow = lax.broadcasted_iota(jnp.int32, (1, w), 1)
    head_halo = jnp.where(krow < radius, NEG_INF, 0.0).astype(F32)
    tail_halo = jnp.where(krow >= radius + tq, NEG_INF, 0.0).astype(F32)
    lane = lax.broadcasted_iota(jnp.int32, (tq, LANES), 1)
    lo_lane = lane < D_HEAD
    m_lo = jnp.where(lo_lane, 1.0, 0.0).astype(BF16)
    m_hi = jnp.where(lo_lane, 0.0, 1.0).astype(BF16)
    scale = D_HEAD ** -0.5

    def sub_block(sb, carry):
        qs = pl.multiple_of(sb * tq, tq)
        no_head = jnp.where(jnp.logical_and(seq_first, sb == 0), 1.0, 0.0).astype(F32)
        no_tail = jnp.where(jnp.logical_and(seq_last, sb == nsb - 1), 1.0, 0.0).astype(F32)
        bias = band_bias + head_halo * no_head + tail_halo * no_tail
        for p in range(n_pairs):
            q2 = q_ref[pl.ds(qs, tq), p * LANES:(p + 1) * LANES]
            if group == 1:
                kv_blk, kv_half = p, None
            else:
                kv_head = (2 * p) // group
                kv_blk, kv_half = kv_head // 2, kv_head % 2
            k2 = kbuf[pl.ds(qs, w), kv_blk * LANES:(kv_blk + 1) * LANES]
            v2 = vbuf[pl.ds(qs, w), kv_blk * LANES:(kv_blk + 1) * LANES]
            if kv_half is None:
                qa, qb = q2 * m_lo, q2 * m_hi
            elif kv_half == 0:
                qa, qb = q2 * m_lo, _swap_halves(q2) * m_lo
            else:
                qa, qb = _swap_halves(q2) * m_hi, q2 * m_hi
            pv, ms, ls = [], [], []
            for hh, qm in enumerate((qa, qb)):
                s = _dot_nt(qm, k2) * scale + bias
                m = jnp.max(s, axis=-1, keepdims=True)
                if has_sink:
                    sk = sink_ref[2 * p + hh]
                    m = jnp.maximum(m, sk)
                pr = jnp.exp(s - m)
                l = jnp.sum(pr, axis=-1, keepdims=True)
                if has_sink:
                    l = l + jnp.exp(sk - m)
                pv.append(_dot(pr.astype(BF16), v2))
                ms.append(m)
                ls.append(l)
            if kv_half is None:
                acc = jnp.where(lo_lane, pv[0], pv[1])
            elif kv_half == 0:
                acc = jnp.where(lo_lane, pv[0], pltpu.roll(pv[1], LANES // 2, 1))
            else:
                acc = jnp.where(lo_lane, pltpu.roll(pv[0], LANES // 2, 1), pv[1])
            den = jnp.where(lo_lane, ls[0], ls[1])
            o_ref[pl.ds(qs, tq), p * LANES:(p + 1) * LANES] = (acc / den).astype(o_ref.dtype)
            if want_lse:
                lse_ref[pl.ds(qs, tq), p * LANES:(p + 1) * LANES] = jnp.where(
                    lo_lane, ms[0] + jnp.log(ls[0]), ms[1] + jnp.log(ls[1]))
        return carry

    lax.fori_loop(0, nsb, sub_block, 0)


def _band_attn_call(qkv, sink, *, seq_len, dil, radius, hq, hkv, q_blk, k_blk, v_blk, want_lse, out_dtype,
                    ch=1024, tq=128):
    rows = qkv.shape[0]
    wq, wkv = hq * D_HEAD, hkv * D_HEAD
    ch = min(ch, seq_len)
    tq = min(tq, ch)
    nchunks = rows // ch
    nhalo = rows // radius
    hpc = ch // radius
    chunks_per_res = nchunks // dil

    def halo_prev(i):
        return jnp.maximum(i * hpc - 1, 0)

    def halo_next(i):
        return jnp.minimum((i + 1) * hpc, nhalo - 1)

    in_specs = [
        pl.BlockSpec((ch, wq), lambda i: (i, q_blk)),
        pl.BlockSpec((ch, wkv), lambda i: (i, k_blk)),
        pl.BlockSpec((radius, wkv), lambda i: (halo_prev(i), k_blk)),
        pl.BlockSpec((radius, wkv), lambda i: (halo_next(i), k_blk)),
        pl.BlockSpec((ch, wkv), lambda i: (i, v_blk)),
        pl.BlockSpec((radius, wkv), lambda i: (halo_prev(i), v_blk)),
        pl.BlockSpec((radius, wkv), lambda i: (halo_next(i), v_blk)),
    ]
    args = [qkv] * 7
    if sink is not None:
        in_specs.append(pl.BlockSpec(memory_space=pltpu.SMEM))
        args.append(sink)
    out_map = lambda i: (i % chunks_per_res, i // chunks_per_res)
    out_shape = [jax.ShapeDtypeStruct((rows // dil, dil * wq), out_dtype)]
    out_specs = [pl.BlockSpec((ch, wq), out_map)]
    if want_lse:
        out_shape.append(jax.ShapeDtypeStruct((rows // dil, dil * wq), F32))
        out_specs.append(pl.BlockSpec((ch, wq), out_map))
    outs = pl.pallas_call(
        functools.partial(_band_attn_kernel, radius=radius, ch=ch, tq=tq, n_pairs=hq // 2, group=hq // hkv,
                          chunks_per_seq=seq_len // ch, has_sink=sink is not None, want_lse=want_lse),
        grid=(nchunks,),
        in_specs=in_specs,
        out_specs=out_specs,
        out_shape=out_shape,
        scratch_shapes=[pltpu.VMEM((ch + 2 * radius, wkv), BF16), pltpu.VMEM((ch + 2 * radius, wkv), BF16)],
        compiler_params=_cparams(1),
        name="band_attn",
    )(*args)
    return [o.reshape(rows, wq) for o in outs]


NA_QROWS = 4


def _na_bias_tables(rpb, grid_rows):
    nq, nk = NA_QROWS * GRID_W, 3 * NA_QROWS * GRID_W
    variants = []
    last = grid_rows // NA_QROWS - 1
    for blk in (0, 1, last):
        q = np.arange(nq)
        qr, qc = blk * NA_QROWS + q // GRID_W, q % GRID_W
        k = np.arange(nk)
        kblk = k // (NA_QROWS * GRID_W)
        kr = (blk - 1 + kblk) * NA_QROWS + (k % (NA_QROWS * GRID_W)) // GRID_W
        kc = k % GRID_W
        exists = ~(((kblk == 0) & (blk == 0)) | ((kblk == 2) & (blk == last)))
        rs = np.clip(qr - NA_ROWS // 2, 0, grid_rows - NA_ROWS)
        cs = np.clip(qc - NA_COLS // 2, 0, GRID_W - NA_COLS)
        valid = (exists[None, :] & (kr[None, :] >= rs[:, None]) & (kr[None, :] < rs[:, None] + NA_ROWS)
                 & (kc[None, :] >= cs[:, None]) & (kc[None, :] < cs[:, None] + NA_COLS))
        ir = np.clip(kr[None, :] - qr[:, None] + NA_ROWS - 1, 0, 2 * NA_ROWS - 2)
        ic = np.clip(kc[None, :] - qc[:, None] + NA_COLS - 1, 0, 2 * NA_COLS - 2)
        variants.append(jnp.where(valid[None], rpb[:, ir, ic].astype(F32), NEG_INF))
    return jnp.stack(variants, axis=0)


def _na_kernel(q_ref, kp_ref, k_ref, kn_ref, vp_ref, v_ref, vn_ref, bias_ref, o_ref, kbuf, vbuf, *, n_pairs):
    nq = q_ref.shape[0]
    kbuf[0:nq, :] = kp_ref[...]
    kbuf[nq:2 * nq, :] = k_ref[...]
    kbuf[2 * nq:3 * nq, :] = kn_ref[...]
    vbuf[0:nq, :] = vp_ref[...]
    vbuf[nq:2 * nq, :] = v_ref[...]
    vbuf[2 * nq:3 * nq, :] = vn_ref[...]
    lane = lax.broadcasted_iota(jnp.int32, (nq, LANES), 1)
    lo_lane = lane < D_HEAD
    m_lo = jnp.where(lo_lane, 1.0, 0.0).astype(BF16)
    m_hi = jnp.where(lo_lane, 0.0, 1.0).astype(BF16)
    for p in range(n_pairs):
        cols = slice(p * LANES, (p + 1) * LANES)
        q2 = q_ref[:, cols]
        k2 = kbuf[:, cols]
        v2 = vbuf[:, cols]
        pv, ls = [], []
        for hh, qm in enumerate((q2 * m_lo, q2 * m_hi)):
            s = _dot_nt(qm, k2) + bias_ref[2 * p + hh]
            m = jnp.max(s, axis=-1, keepdims=True)
            pr = jnp.exp(s - m)
            ls.append(jnp.sum(pr, axis=-1, keepdims=True))
            pv.append(_dot(pr.astype(BF16), v2))
        acc = jnp.where(lo_lane, pv[0], pv[1])
        den = jnp.where(lo_lane, ls[0], ls[1])
        o_ref[:, cols] = (acc / den).astype(o_ref.dtype)


def _na_call(qkv, bias, *, batch, seq, heads):
    rows = qkv.shape[0]
    wq = heads * D_HEAD
    nq = NA_QROWS * GRID_W
    nblk = seq // nq

    def clamp_blk(b, j):
        return b * nblk + jnp.clip(j, 0, nblk - 1)

    def variant(j):
        return jnp.where(j == 0, 0, jnp.where(j == nblk - 1, 2, 1))

    blk = (nq, wq)
    in_specs = [
        pl.BlockSpec(blk, lambda b, j: (b * nblk + j, 0)),
        pl.BlockSpec(blk, lambda b, j: (clamp_blk(b, j - 1), 1)),
        pl.BlockSpec(blk, lambda b, j: (b * nblk + j, 1)),
        pl.BlockSpec(blk, lambda b, j: (clamp_blk(b, j + 1), 1)),
        pl.BlockSpec(blk, lambda b, j: (clamp_blk(b, j - 1), 2)),
        pl.BlockSpec(blk, lambda b, j: (b * nblk + j, 2)),
        pl.BlockSpec(blk, lambda b, j: (clamp_blk(b, j + 1), 2)),
        pl.BlockSpec((None, heads, nq, 3 * nq), lambda b, j: (variant(j), 0, 0, 0)),
    ]
    return pl.pallas_call(
        functools.partial(_na_kernel, n_pairs=heads // 2),
        grid=(batch, nblk),
        in_specs=in_specs,
        out_specs=pl.BlockSpec(blk, lambda b, j: (b * nblk + j, 0)),
        out_shape=jax.ShapeDtypeStruct((rows, wq), BF16),
        scratch_shapes=[pltpu.VMEM((3 * nq, wq), BF16), pltpu.VMEM((3 * nq, wq), BF16)],
        compiler_params=_cparams(2),
        name="na_attn",
    )(*([qkv] * 7), bias)


def _mix_out_kernel(*refs, n_groups, g_row):
    if n_groups > 1:
        o_refs = refs[:n_groups]
        l_refs = refs[n_groups:2 * n_groups]
        h_ref, g_ref, wo_ref, out_ref = refs[2 * n_groups:]
        lses = [r[...] for r in l_refs]
        mx = lses[0]
        for l in lses[1:]:
            mx = jnp.maximum(mx, l)
        es = [jnp.exp(l - mx) for l in lses]
        den = es[0]
        for e in es[1:]:
            den = den + e
        o = (es[0] / den) * o_refs[0][...]
        for e, r in zip(es[1:], o_refs[1:]):
            o = o + (e / den) * r[...]
        o = o.astype(BF16)
    else:
        o_ref, h_ref, g_ref, wo_ref, out_ref = refs
        o = o_ref[...]
    y = _dot(o, wo_ref[...])
    out_ref[...] = h_ref[...] + _rms(y, g_ref[g_row:g_row + 1, :])


def _mix_out_call(os_, lses, h, g8, wo, *, g_row, tm):
    rows, d = h.shape
    wq = os_[0].shape[1]
    tile = lambda width: pl.BlockSpec((tm, width), lambda i: (i, 0))
    args = list(os_) + list(lses) + [h, g8, wo]
    in_specs = [tile(wq)] * (len(os_) + len(lses)) + [tile(d), _resident(g8.shape), _resident(wo.shape)]
    return pl.pallas_call(
        functools.partial(_mix_out_kernel, n_groups=len(os_), g_row=g_row),
        grid=(rows // tm,),
        in_specs=in_specs,
        out_specs=tile(d),
        out_shape=jax.ShapeDtypeStruct(h.shape, F32),
        compiler_params=_cparams(1),
        name="mix_out",
    )(*args)


def _ple_kernel(h_ref, p_ref, g_ref, proj_ref, gate_ref, o_ref, *, g_row):
    h = h_ref[...]
    e = _dot(p_ref[...].astype(BF16), proj_ref[...])
    hn = _rms(h, g_ref[g_row:g_row + 1, :]).astype(BF16)
    gate = jax.nn.sigmoid(_dot(hn, gate_ref[...]))
    o_ref[...] = h + _rms(e * gate, g_ref[g_row + 1:g_row + 2, :])


def _ple_call(h, p, g8, proj, gate, *, g_row, tm):
    rows, d = h.shape
    pd = p.shape[1]
    return pl.pallas_call(
        functools.partial(_ple_kernel, g_row=g_row),
        grid=(rows // tm,),
        in_specs=[pl.BlockSpec((tm, d), lambda i: (i, 0)), pl.BlockSpec((tm, pd), lambda i: (i, 0)),
                  _resident(g8.shape), _resident(proj.shape), _resident(gate.shape)],
        out_specs=pl.BlockSpec((tm, d), lambda i: (i, 0)),
        out_shape=jax.ShapeDtypeStruct(h.shape, F32),
        compiler_params=_cparams(1),
        name="ple",
    )(h, p, g8, proj, gate)


def _rope_lane_tables(seq):
    half = ROT_DIM // 2
    pos = jnp.arange(seq, dtype=F32)
    inv = ROPE_THETA ** (-jnp.arange(0, ROT_DIM, 2, dtype=F32) / ROT_DIM)
    ang = pos[:, None] * inv[None, :]
    c, s = jnp.cos(ang), jnp.sin(ang)
    ones = jnp.ones((seq, D_HEAD - ROT_DIM), F32)
    z = lambda n: jnp.zeros((seq, n), F32)
    cf = jnp.concatenate([c, c, ones], axis=1)
    sa = jnp.concatenate([-s, z(D_HEAD - half)], axis=1)
    sb = jnp.concatenate([z(half), s, z(D_HEAD - ROT_DIM)], axis=1)
    return [jnp.concatenate([t, t], axis=1) for t in (cf, sa, sb)]


def kernel(x, p, norm_g, ffn_wi, ffn_wo, ple_proj, ple_gate, a_wqkv, a_wo, b_wqkv, b_wo, b_sink, c_wqkv, c_wo, c_rpb):
    batch, seq, d = x.shape
    depth = norm_g.shape[0]
    rows = batch * seq
    tm = 512
    tables = _rope_lane_tables(seq)
    ng = len(A_PAIRS)
    wa = A_HEADS * D_HEAD

    h = x.reshape(rows, d)
    for i in range(depth):
        g8 = norm_g[i]
        h = _ffn_call(h, g8, ffn_wi[i, 0].astype(BF16), ffn_wo[i, 0].astype(BF16), g_row=0, tm=tm)
        mixer, j = i % N_MIXERS, i // N_MIXERS
        if mixer == 0:
            w4 = a_wqkv[j].astype(BF16).reshape(d, 3, ng, wa)
            os_, lses = [], []
            for gi, (window, dil) in enumerate(A_PAIRS):
                qkv = _proj_call(h, g8, w4[:, :, gi, :].reshape(d, 3 * wa), tables, seq=seq, dil=dil, g_row=2,
                                 n_rope=2 * wa, n_scaled=0, scale=1.0, tm=tm)
                o, lse = _band_attn_call(qkv, None, seq_len=seq // dil, dil=dil, radius=window // (2 * dil),
                                         hq=A_HEADS, hkv=A_HEADS, q_blk=0, k_blk=1, v_blk=2, want_lse=True,
                                         out_dtype=F32)
                os_.append(o)
                lses.append(lse)
            h = _mix_out_call(os_, lses, h, g8, a_wo[j].astype(BF16), g_row=3, tm=tm)
        elif mixer == 1:
            wq, wkv = B_HEADS * D_HEAD, B_KV_HEADS * D_HEAD
            qkv = _proj_call(h, g8, b_wqkv[j].astype(BF16), tables, seq=seq, dil=1, g_row=2, n_rope=wq + wkv,
                             n_scaled=0, scale=1.0, tm=tm)
            (o,) = _band_attn_call(qkv, b_sink[j].astype(F32), seq_len=seq, dil=1, radius=B_RADIUS, hq=B_HEADS,
                                   hkv=B_KV_HEADS, q_blk=0, k_blk=wq // wkv, v_blk=wq // wkv + 1, want_lse=False,
                                   out_dtype=BF16)
            h = _mix_out_call([o], [], h, g8, b_wo[j].astype(BF16), g_row=3, tm=tm)
        else:
            wq = C_HEADS * D_HEAD
            qkv = _proj_call(h, g8, c_wqkv[j].astype(BF16), None, seq=seq, dil=1, g_row=2, n_rope=0, n_scaled=wq,
                             scale=D_HEAD ** -0.5, tm=tm)
            bias = _na_bias_tables(c_rpb[j], seq // GRID_W)
            o = _na_call(qkv, bias, batch=batch, seq=seq, heads=C_HEADS)
            h = _mix_out_call([o], [], h, g8, c_wo[j].astype(BF16), g_row=3, tm=tm)
        h = _ffn_call(h, g8, ffn_wi[i, 1].astype(BF16), ffn_wo[i, 1].astype(BF16), g_row=4, tm=tm)
        h = _ple_call(h, p[i].reshape(rows, -1), g8, ple_proj[i].astype(BF16), ple_gate[i].astype(BF16), g_row=6,
                      tm=tm)
    return h.reshape(batch, seq, d)
```

```python
import functools

import numpy as np
import jax
import jax.numpy as jnp
from jax import lax
from jax.experimental import pallas as pl
from jax.experimental.pallas import tpu as pltpu

F32 = jnp.float32
BF16 = jnp.bfloat16

D_HEAD = 64
ROT_DIM = D_HEAD // 4
ROPE_THETA = 500000.0
RMS_EPS = 1e-6
NEG_INF = -1e30
N_MIXERS = 3
A_PAIRS = ((128, 1), (512, 4), (2048, 16))
A_HEADS = 8
B_HEADS = 16
B_KV_HEADS = 4
B_RADIUS = 128
C_HEADS = 16
GRID_W = 64
NA_ROWS = 8
NA_COLS = 16
NA_QROWS = 4

LANES = 128
VMEM_LIMIT_BYTES = 58 * 1024 * 1024
LOG2E = 1.4426950408889634
LN2 = 0.6931471805599453
FFN_CHUNK = 256
TOKEN_TILE = 512
ATTN_CHUNK = 1024
ATTN_QBLOCK = 128


def _cparams(n_axes):
    return pltpu.CompilerParams(dimension_semantics=("arbitrary",) * n_axes,
                                vmem_limit_bytes=VMEM_LIMIT_BYTES)


def _resident(shape):
    zeros = (0,) * len(shape)
    return pl.BlockSpec(shape, lambda *_: zeros, pipeline_mode=pl.Buffered(1))


def _rms(x, g):
    ms = jnp.mean(x * x, axis=-1, keepdims=True)
    return x * lax.rsqrt(ms + RMS_EPS) * g


def _dot(a, b):
    return jnp.dot(a, b, preferred_element_type=F32)


def _dot_nt(a, b):
    return lax.dot_general(a, b, (((1,), (1,)), ((), ())), preferred_element_type=F32)


def _ffn(h, g_ref, g_row, wi_ref, wo_ref):
    d_ff = wo_ref.shape[0]
    xn = _rms(h, g_ref[g_row:g_row + 1, :]).astype(BF16)
    acc = jnp.zeros(h.shape, F32)
    for c0 in range(0, d_ff, FFN_CHUNK):
        gate = _dot(xn, wi_ref[:, c0:c0 + FFN_CHUNK])
        up = _dot(xn, wi_ref[:, d_ff + c0:d_ff + c0 + FFN_CHUNK])
        act = (gate * jax.nn.sigmoid(gate)) * up
        acc = acc + _dot(act.astype(BF16), wo_ref[c0:c0 + FFN_CHUNK, :])
    return h + 0.5 * _rms(acc, g_ref[g_row + 1:g_row + 2, :])


def _deinterleave(ref, dil):
    if ref.ndim == 2:
        n = ref.shape[0] // dil
        return jnp.concatenate([ref[pl.ds(r, n, stride=dil), :] for r in range(dil)], axis=0)
    n = ref.shape[1] // dil
    return jnp.concatenate(
        [jnp.concatenate([ref[c, pl.ds(r, n, stride=dil), :] for r in range(dil)], axis=0)
         for c in range(ref.shape[0])], axis=1)


def _ffn_proj_kernel(*refs, groups, n_rope, n_scaled, scale, use_rope, need_scratch):
    it = iter(refs)
    h_ref, g_ref, wi_ref, wo_ref, w_ref = [next(it) for _ in range(5)]
    tab_refs = [next(it) for _ in range(3)] if use_rope else []
    hout_ref = next(it)
    out_refs = [next(it) for _ in groups]
    xn_s = next(it) if need_scratch else None

    h1 = _ffn(h_ref[...], g_ref, 0, wi_ref, wo_ref)
    hout_ref[...] = h1
    xn = _rms(h1, g_ref[2:3, :])
    if need_scratch:
        for c in range(xn_s.shape[0]):
            xn_s[c] = xn[:, c * LANES:(c + 1) * LANES]
    tm = h1.shape[0]
    for (dil, col0, ncols), o_ref in zip(groups, out_refs):
        tmd = tm // dil
        xp = (xn if dil == 1 else _deinterleave(xn_s, dil)).astype(BF16)
        tabs = [t[...] if dil == 1 else _deinterleave(t, dil) for t in tab_refs]
        for c0 in range(0, ncols, FFN_CHUNK):
            y = _dot(xp, w_ref[:, col0 + c0:col0 + c0 + FFN_CHUNK])
            for j0 in range(0, FFN_CHUNK, LANES):
                yc = y[:, j0:j0 + LANES]
                if c0 + j0 < n_rope:
                    cf, sa, sb = tabs
                    yc = (yc * cf + pltpu.roll(yc, LANES - ROT_DIM // 2, 1) * sa
                          + pltpu.roll(yc, ROT_DIM // 2, 1) * sb)
                if c0 + j0 < n_scaled:
                    yc = yc * scale
                yc = yc.astype(BF16)
                for r in range(dil):
                    o_ref[r, :, c0 + j0:c0 + j0 + LANES] = yc[r * tmd:(r + 1) * tmd]


def _ffn_proj_call(h, g8, wi, wo, w, tables, *, groups, n_rope, n_scaled, scale, tm):
    rows, d = h.shape
    tile = lambda width: pl.BlockSpec((tm, width), lambda i: (i, 0))
    use_rope = n_rope > 0
    need_scratch = any(dil > 1 for dil, _, _ in groups)
    in_specs = [tile(d), _resident(g8.shape), _resident(wi.shape), _resident(wo.shape), _resident(w.shape)]
    args = [h, g8, wi, wo, w]
    if use_rope:
        seq_tiles = tables[0].shape[0] // tm
        in_specs += [pl.BlockSpec((tm, LANES), lambda i: (i % seq_tiles, 0))] * 3
        args += list(tables)
    out_shape = [jax.ShapeDtypeStruct((rows, d), F32)]
    out_specs = [tile(d)]
    for dil, _, ncols in groups:
        out_shape.append(jax.ShapeDtypeStruct((dil, rows // dil, ncols), BF16))
        out_specs.append(pl.BlockSpec((dil, tm // dil, ncols), lambda i: (0, i, 0)))
    outs = pl.pallas_call(
        functools.partial(_ffn_proj_kernel, groups=tuple(groups), n_rope=n_rope, n_scaled=n_scaled, scale=scale,
                          use_rope=use_rope, need_scratch=need_scratch),
        grid=(rows // tm,),
        in_specs=in_specs,
        out_specs=out_specs,
        out_shape=out_shape,
        scratch_shapes=[pltpu.VMEM((d // LANES, tm, LANES), F32)] if need_scratch else [],
        compiler_params=_cparams(1),
        name="ffn_proj",
    )(*args)
    return outs[0], [o.reshape(rows, o.shape[2]) for o in outs[1:]]


def _swap_halves(x):
    half = LANES // 2
    return jnp.concatenate([x[:, half:], x[:, :half]], axis=1)


def _band_attn_kernel(*refs, radius, ch, tq, hq, group, chunks_per_seq, has_sink, want_lse):
    it = iter(refs)
    q_ref, k_ref, kp_ref, kn_ref, v_ref, vp_ref, vn_ref = [next(it) for _ in range(7)]
    sink_ref = next(it) if has_sink else None
    o_ref = next(it)
    lse_ref = next(it) if want_lse else None
    kbuf, vbuf = next(it), next(it)

    w = tq + 2 * radius
    nsb = ch // tq
    kbuf[0:radius, :] = kp_ref[...]
    kbuf[radius:radius + ch, :] = k_ref[...]
    kbuf[radius + ch:radius + ch + radius, :] = kn_ref[...]
    vbuf[0:radius, :] = vp_ref[...]
    vbuf[radius:radius + ch, :] = v_ref[...]
    vbuf[radius + ch:radius + ch + radius, :] = vn_ref[...]

    cidx = pl.program_id(0) % chunks_per_seq
    seq_first = cidx == 0
    seq_last = cidx == chunks_per_seq - 1

    qi = lax.broadcasted_iota(jnp.int32, (tq, w), 0)
    ki = lax.broadcasted_iota(jnp.int32, (tq, w), 1)
    rel = ki - qi
    band_bias = jnp.where((rel >= 0) & (rel <= 2 * radius), 0.0, NEG_INF).astype(F32)
    krow = lax.broadcasted_iota(jnp.int32, (1, w), 1)
    head_halo = jnp.where(krow < radius, NEG_INF, 0.0).astype(F32)
    tail_halo = jnp.where(krow >= radius + tq, NEG_INF, 0.0).astype(F32)
    lane = lax.broadcasted_iota(jnp.int32, (tq, LANES), 1)
    lo_lane = lane < D_HEAD
    m_lo = jnp.where(lo_lane, 1.0, 0.0).astype(BF16)
    m_hi = jnp.where(lo_lane, 0.0, 1.0).astype(BF16)
    c_log2 = D_HEAD ** -0.5 * LOG2E
    heads_per_blk = 2
    n_blk = hq // heads_per_blk

    def sub_block(sb, carry):
        qs = pl.multiple_of(sb * tq, tq)
        no_head = jnp.where(jnp.logical_and(seq_first, sb == 0), 1.0, 0.0).astype(F32)
        no_tail = jnp.where(jnp.logical_and(seq_last, sb == nsb - 1), 1.0, 0.0).astype(F32)
        bias = band_bias + head_halo * no_head + tail_halo * no_tail
        for b in range(n_blk):
            if group == 1:
                kv_blk, kv_half = b, None
                q2 = q_ref[pl.ds(qs, tq), b * LANES:(b + 1) * LANES]
                parts = [q2 * m_lo, q2 * m_hi]
            else:
                kv_head = (2 * b) // group
                kv_blk, kv_half = kv_head // 2, kv_head % 2
                q2 = q_ref[pl.ds(qs, tq), b * LANES:(b + 1) * LANES]
                if kv_half == 0:
                    parts = [q2 * m_lo, _swap_halves(q2) * m_lo]
                else:
                    parts = [_swap_halves(q2) * m_hi, q2 * m_hi]
            k2 = kbuf[pl.ds(qs, w), kv_blk * LANES:(kv_blk + 1) * LANES]
            v2 = vbuf[pl.ds(qs, w), kv_blk * LANES:(kv_blk + 1) * LANES]
            nh = len(parts)
            s = _dot_nt(jnp.concatenate(parts, axis=0), k2)
            s = (s.reshape(nh, tq, w) * c_log2 + bias[None]).reshape(nh * tq, w)
            m = jnp.max(s, axis=-1, keepdims=True)
            if has_sink:
                sks = [sink_ref[b * heads_per_blk + hh] * LOG2E for hh in range(nh)]
                m = jnp.concatenate([jnp.maximum(m[hh * tq:(hh + 1) * tq], sks[hh]) for hh in range(nh)], axis=0)
            pr = jnp.exp2(s - m)
            l = jnp.sum(pr, axis=-1, keepdims=True)
            if has_sink:
                l = l + jnp.concatenate([jnp.exp2(sks[hh] - m[hh * tq:(hh + 1) * tq]) for hh in range(nh)], axis=0)
            pv = _dot(pr.astype(BF16), v2)
            for t in range(nh // 2):
                pair = b * (nh // 2) + t
                ra, rb = slice(2 * t * tq, (2 * t + 1) * tq), slice((2 * t + 1) * tq, (2 * t + 2) * tq)
                if kv_half is None:
                    acc = jnp.where(lo_lane, pv[ra], pv[rb])
                elif kv_half == 0:
                    acc = jnp.where(lo_lane, pv[ra], pltpu.roll(pv[rb], LANES // 2, 1))
                else:
                    acc = jnp.where(lo_lane, pltpu.roll(pv[ra], LANES // 2, 1), pv[rb])
                den = jnp.where(lo_lane, l[ra], l[rb])
                o_ref[pl.ds(qs, tq), pair * LANES:(pair + 1) * LANES] = (acc / den).astype(o_ref.dtype)
                if want_lse:
                    lse_ref[pl.ds(qs, tq), pair * LANES:(pair + 1) * LANES] = jnp.where(
                        lo_lane, m[ra] * LN2 + jnp.log(l[ra]), m[rb] * LN2 + jnp.log(l[rb]))
        return carry

    lax.fori_loop(0, nsb, sub_block, 0)


def _band_attn_call(qkv, sink, *, seq_len, radius, hq, hkv, q_blk, k_blk, v_blk, want_lse, out_dtype):
    rows = qkv.shape[0]
    wq, wkv = hq * D_HEAD, hkv * D_HEAD
    ch = min(ATTN_CHUNK, seq_len)
    tq = min(ATTN_QBLOCK, ch)
    nchunks = rows // ch
    nhalo = rows // radius
    hpc = ch // radius

    def halo_prev(i):
        return jnp.maximum(i * hpc - 1, 0)

    def halo_next(i):
        return jnp.minimum((i + 1) * hpc, nhalo - 1)

    in_specs = [
        pl.BlockSpec((ch, wq), lambda i: (i, q_blk)),
        pl.BlockSpec((ch, wkv), lambda i: (i, k_blk)),
        pl.BlockSpec((radius, wkv), lambda i: (halo_prev(i), k_blk)),
        pl.BlockSpec((radius, wkv), lambda i: (halo_next(i), k_blk)),
        pl.BlockSpec((ch, wkv), lambda i: (i, v_blk)),
        pl.BlockSpec((radius, wkv), lambda i: (halo_prev(i), v_blk)),
        pl.BlockSpec((radius, wkv), lambda i: (halo_next(i), v_blk)),
    ]
    args = [qkv] * 7
    if sink is not None:
        in_specs.append(pl.BlockSpec(memory_space=pltpu.SMEM))
        args.append(sink)
    out_shape = [jax.ShapeDtypeStruct((rows, wq), out_dtype)]
    out_specs = [pl.BlockSpec((ch, wq), lambda i: (i, 0))]
    if want_lse:
        out_shape.append(jax.ShapeDtypeStruct((rows, wq), F32))
        out_specs.append(pl.BlockSpec((ch, wq), lambda i: (i, 0)))
    return pl.pallas_call(
        functools.partial(_band_attn_kernel, radius=radius, ch=ch, tq=tq, hq=hq, group=hq // hkv,
                          chunks_per_seq=seq_len // ch, has_sink=sink is not None, want_lse=want_lse),
        grid=(nchunks,),
        in_specs=in_specs,
        out_specs=out_specs,
        out_shape=out_shape,
        scratch_shapes=[pltpu.VMEM((ch + 2 * radius, wkv), BF16), pltpu.VMEM((ch + 2 * radius, wkv), BF16)],
        compiler_params=_cparams(1),
        name="band_attn",
    )(*args)


def _na_bias_tables(rpb, grid_rows):
    heads = rpb.shape[0]
    nblk = grid_rows // NA_QROWS
    qc = np.arange(GRID_W)
    kc = np.arange(GRID_W)
    cs = np.clip(qc - NA_COLS // 2, 0, GRID_W - NA_COLS)
    col_ok = (kc[None, :] >= cs[:, None]) & (kc[None, :] < cs[:, None] + NA_COLS)
    ic = np.clip(kc[None, :] - qc[:, None] + NA_COLS - 1, 0, 2 * NA_COLS - 2)
    col_tiles = jnp.where(col_ok[None, None], rpb.astype(F32)[:, :, ic] * LOG2E, NEG_INF)
    masked = jnp.full((heads, 1, GRID_W, GRID_W), NEG_INF, F32)
    col_tiles = jnp.concatenate([col_tiles, masked], axis=1)
    variants = []
    for blk in (0, 1, nblk - 1):
        qr = blk * NA_QROWS + np.arange(NA_QROWS)
        kb = np.arange(3 * NA_QROWS) // NA_QROWS
        kr = (blk - 1) * NA_QROWS + np.arange(3 * NA_QROWS)
        exists = ~(((kb == 0) & (blk == 0)) | ((kb == 2) & (blk == nblk - 1)))
        rs = np.clip(qr - NA_ROWS // 2, 0, grid_rows - NA_ROWS)
        row_ok = exists[None, :] & (kr[None, :] >= rs[:, None]) & (kr[None, :] < rs[:, None] + NA_ROWS)
        ir = np.where(row_ok, kr[None, :] - qr[:, None] + NA_ROWS - 1, 2 * NA_ROWS - 1)
        t = col_tiles[:, ir]
        t = t.transpose(0, 1, 3, 2, 4).reshape(heads, NA_QROWS * GRID_W, 3 * NA_QROWS * GRID_W)
        variants.append(t)
    return jnp.stack(variants, axis=0)


def _na_kernel(q_ref, kp_ref, k_ref, kn_ref, vp_ref, v_ref, vn_ref, bias_ref, o_ref, kbuf, vbuf, *, n_pairs):
    nq = q_ref.shape[0]
    kbuf[0:nq, :] = kp_ref[...]
    kbuf[nq:2 * nq, :] = k_ref[...]
    kbuf[2 * nq:3 * nq, :] = kn_ref[...]
    vbuf[0:nq, :] = vp_ref[...]
    vbuf[nq:2 * nq, :] = v_ref[...]
    vbuf[2 * nq:3 * nq, :] = vn_ref[...]
    lane = lax.broadcasted_iota(jnp.int32, (nq, LANES), 1)
    lo_lane = lane < D_HEAD
    m_lo = jnp.where(lo_lane, 1.0, 0.0).astype(BF16)
    m_hi = jnp.where(lo_lane, 0.0, 1.0).astype(BF16)
    for p in range(n_pairs):
        cols = slice(p * LANES, (p + 1) * LANES)
        q2 = q_ref[:, cols]
        s = _dot_nt(jnp.concatenate([q2 * m_lo, q2 * m_hi], axis=0), kbuf[:, cols])
        s = s * LOG2E + jnp.concatenate([bias_ref[2 * p], bias_ref[2 * p + 1]], axis=0)
        m = jnp.max(s, axis=-1, keepdims=True)
        pr = jnp.exp2(s - m)
        l = jnp.sum(pr, axis=-1, keepdims=True)
        pv = _dot(pr.astype(BF16), vbuf[:, cols])
        acc = jnp.where(lo_lane, pv[:nq], pv[nq:])
        den = jnp.where(lo_lane, l[:nq], l[nq:])
        o_ref[:, cols] = (acc / den).astype(o_ref.dtype)


def _na_call(qkv, bias, *, batch, seq, heads):
    rows = qkv.shape[0]
    wq = heads * D_HEAD
    nq = NA_QROWS * GRID_W
    nblk = seq // nq

    def clamp_blk(b, j):
        return b * nblk + jnp.clip(j, 0, nblk - 1)

    def variant(j):
        return jnp.where(j == 0, 0, jnp.where(j == nblk - 1, 2, 1))

    blk = (nq, wq)
    in_specs = [
        pl.BlockSpec(blk, lambda b, j: (b * nblk + j, 0)),
        pl.BlockSpec(blk, lambda b, j: (clamp_blk(b, j - 1), 1)),
        pl.BlockSpec(blk, lambda b, j: (b * nblk + j, 1)),
        pl.BlockSpec(blk, lambda b, j: (clamp_blk(b, j + 1), 1)),
        pl.BlockSpec(blk, lambda b, j: (clamp_blk(b, j - 1), 2)),
        pl.BlockSpec(blk, lambda b, j: (b * nblk + j, 2)),
        pl.BlockSpec(blk, lambda b, j: (clamp_blk(b, j + 1), 2)),
        pl.BlockSpec((None, heads, nq, 3 * nq), lambda b, j: (variant(j), 0, 0, 0)),
    ]
    return pl.pallas_call(
        functools.partial(_na_kernel, n_pairs=heads // 2),
        grid=(batch, nblk),
        in_specs=in_specs,
        out_specs=pl.BlockSpec(blk, lambda b, j: (b * nblk + j, 0)),
        out_shape=jax.ShapeDtypeStruct((rows, wq), BF16),
        scratch_shapes=[pltpu.VMEM((3 * nq, wq), BF16), pltpu.VMEM((3 * nq, wq), BF16)],
        compiler_params=_cparams(2),
        name="na_attn",
    )(*([qkv] * 7), bias)


def _interleave(src_ref, dst_ref, dil):
    if dil == 1:
        return src_ref[0]
    n = src_ref.shape[1]
    for r in range(dil):
        blk = src_ref[r]
        for c in range(dst_ref.shape[0]):
            dst_ref[c, pl.ds(r, n, stride=dil), :] = blk[:, c * LANES:(c + 1) * LANES]
    return jnp.concatenate([dst_ref[c] for c in range(dst_ref.shape[0])], axis=1)


def _mix_ffn_ple_kernel(*refs, dils):
    ng = len(dils)
    it = iter(refs)
    o_refs = [next(it) for _ in range(ng)]
    l_refs = [next(it) for _ in range(ng)] if ng > 1 else []
    h_ref, p_ref, g_ref, wmix_ref, wi_ref, wo_ref, proj_ref, gate_ref, out_ref = [next(it) for _ in range(9)]

    if ng > 1:
        os_, lses = [], []
        for gi, dil in enumerate(dils):
            os_.append(_interleave(o_refs[gi], next(it) if dil > 1 else None, dil))
            lses.append(_interleave(l_refs[gi], next(it) if dil > 1 else None, dil))
        mx = lses[0]
        for l in lses[1:]:
            mx = jnp.maximum(mx, l)
        es = [jnp.exp(l - mx) for l in lses]
        den = es[0]
        for e in es[1:]:
            den = den + e
        o = (es[0] / den) * os_[0]
        for e, og in zip(es[1:], os_[1:]):
            o = o + (e / den) * og
        o = o.astype(BF16)
    else:
        o = o_refs[0][...]
    h = h_ref[...] + _rms(_dot(o, wmix_ref[...]), g_ref[3:4, :])
    h = _ffn(h, g_ref, 4, wi_ref, wo_ref)
    e = _dot(p_ref[...].astype(BF16), proj_ref[...])
    gate = jax.nn.sigmoid(_dot(_rms(h, g_ref[6:7, :]).astype(BF16), gate_ref[...]))
    out_ref[...] = h + _rms(e * gate, g_ref[7:8, :])


def _mix_ffn_ple_call(os_, lses, dils, h, p, g8, wmix, wi, wo, proj, gate, *, tm):
    rows, d = h.shape
    tile = lambda width: pl.BlockSpec((tm, width), lambda i: (i, 0))
    args, in_specs, scratch = [], [], []
    if len(os_) > 1:
        for part in (os_, lses):
            for arr, dil in zip(part, dils):
                wq = arr.shape[1]
                args.append(arr.reshape(dil, rows // dil, wq))
                in_specs.append(pl.BlockSpec((dil, tm // dil, wq), lambda i: (0, i, 0)))
        for arr, dil in zip(os_, dils):
            if dil > 1:
                scratch += [pltpu.VMEM((arr.shape[1] // LANES, tm, LANES), F32)] * 2
    else:
        args.append(os_[0])
        in_specs.append(tile(os_[0].shape[1]))
    args += [h, p, g8, wmix, wi, wo, proj, gate]
    in_specs += [tile(d), tile(p.shape[1]), _resident(g8.shape), _resident(wmix.shape), _resident(wi.shape),
                 _resident(wo.shape), _resident(proj.shape), _resident(gate.shape)]
    return pl.pallas_call(
        functools.partial(_mix_ffn_ple_kernel, dils=tuple(dils)),
        grid=(rows // tm,),
        in_specs=in_specs,
        out_specs=tile(d),
        out_shape=jax.ShapeDtypeStruct(h.shape, F32),
        scratch_shapes=scratch,
        compiler_params=_cparams(1),
        name="mix_ffn_ple",
    )(*args)


def _rope_lane_tables(seq):
    half = ROT_DIM // 2
    pos = jnp.arange(seq, dtype=F32)
    inv = ROPE_THETA ** (-jnp.arange(0, ROT_DIM, 2, dtype=F32) / ROT_DIM)
    ang = pos[:, None] * inv[None, :]
    c, s = jnp.cos(ang), jnp.sin(ang)
    ones = jnp.ones((seq, D_HEAD - ROT_DIM), F32)
    z = lambda n: jnp.zeros((seq, n), F32)
    cf = jnp.concatenate([c, c, ones], axis=1)
    sa = jnp.concatenate([-s, z(D_HEAD - half)], axis=1)
    sb = jnp.concatenate([z(half), s, z(D_HEAD - ROT_DIM)], axis=1)
    return [jnp.concatenate([t, t], axis=1) for t in (cf, sa, sb)]


def kernel(x, p, norm_g, ffn_wi, ffn_wo, ple_proj, ple_gate, a_wqkv, a_wo, b_wqkv, b_wo, b_sink, c_wqkv, c_wo, c_rpb):
    batch, seq, d = x.shape
    depth = norm_g.shape[0]
    rows = batch * seq
    tm = min(TOKEN_TILE, seq)
    tables = _rope_lane_tables(seq)
    ng = len(A_PAIRS)
    wa = A_HEADS * D_HEAD
    bf = lambda a: a.astype(BF16)

    h = x.reshape(rows, d)
    for i in range(depth):
        g8 = norm_g[i]
        mixer, j = i % N_MIXERS, i // N_MIXERS
        ffn1 = (bf(ffn_wi[i, 0]), bf(ffn_wo[i, 0]))
        if mixer == 0:
            w = bf(a_wqkv[j]).reshape(d, 3, ng, wa).transpose(0, 2, 1, 3).reshape(d, ng * 3 * wa)
            groups = [(dil, gi * 3 * wa, 3 * wa) for gi, (_, dil) in enumerate(A_PAIRS)]
            h, qkvs = _ffn_proj_call(h, g8, *ffn1, w, tables, groups=groups, n_rope=2 * wa, n_scaled=0, scale=1.0,
                                     tm=tm)
            os_, lses = [], []
            for qkv, (window, dil) in zip(qkvs, A_PAIRS):
                o, lse = _band_attn_call(qkv, None, seq_len=seq // dil, radius=window // (2 * dil), hq=A_HEADS,
                                         hkv=A_HEADS, q_blk=0, k_blk=1, v_blk=2, want_lse=True, out_dtype=F32)
                os_.append(o)
                lses.append(lse)
            dils = [dil for _, dil in A_PAIRS]
            wmix = a_wo[j]
        elif mixer == 1:
            wq, wkv = B_HEADS * D_HEAD, B_KV_HEADS * D_HEAD
            h, (qkv,) = _ffn_proj_call(h, g8, *ffn1, bf(b_wqkv[j]), tables, groups=[(1, 0, wq + 2 * wkv)],
                                       n_rope=wq + wkv, n_scaled=0, scale=1.0, tm=tm)
            os_ = _band_attn_call(qkv, b_sink[j].astype(F32), seq_len=seq, radius=B_RADIUS, hq=B_HEADS,
                                  hkv=B_KV_HEADS, q_blk=0, k_blk=wq // wkv, v_blk=wq // wkv + 1, want_lse=False,
                                  out_dtype=BF16)
            lses, dils, wmix = [], [1], b_wo[j]
        else:
            wq = C_HEADS * D_HEAD
            h, (qkv,) = _ffn_proj_call(h, g8, *ffn1, bf(c_wqkv[j]), None, groups=[(1, 0, 3 * wq)], n_rope=0,
                                       n_scaled=wq, scale=D_HEAD ** -0.5, tm=tm)
            bias = _na_bias_tables(c_rpb[j], seq // GRID_W)
            os_ = [_na_call(qkv, bias, batch=batch, seq=seq, heads=C_HEADS)]
            lses, dils, wmix = [], [1], c_wo[j]
        h = _mix_ffn_ple_call(os_, lses, dils, h, p[i].reshape(rows, -1), g8, bf(wmix), bf(ffn_wi[i, 1]),
                              bf(ffn_wo[i, 1]), bf(ple_proj[i]), bf(ple_gate[i]), tm=tm)
    return h.reshape(batch, seq, d)
```

```python
import functools

import numpy as np
import jax
import jax.numpy as jnp
from jax import lax
from jax.experimental import pallas as pl
from jax.experimental.pallas import tpu as pltpu

F32 = jnp.float32
BF16 = jnp.bfloat16

D_HEAD = 64
ROT_DIM = D_HEAD // 4
ROPE_THETA = 500000.0
RMS_EPS = 1e-6
NEG_INF = -1e30
N_MIXERS = 3
A_PAIRS = ((128, 1), (512, 4), (2048, 16))
A_HEADS = 8
B_HEADS = 16
B_KV_HEADS = 4
B_RADIUS = 128
C_HEADS = 16
GRID_W = 64
NA_ROWS = 8
NA_COLS = 16
NA_QROWS = 4

LANES = 128
VMEM_LIMIT_BYTES = 58 * 1024 * 1024
LOG2E = 1.4426950408889634
LN2 = 0.6931471805599453
FFN_CHUNK = 256
TOKEN_TILE = 512
SUB_TILES = 1
ATTN_CHUNK = 1024
ATTN_QBLOCK = 128


def _cparams(n_axes):
    return pltpu.CompilerParams(dimension_semantics=("arbitrary",) * n_axes,
                                vmem_limit_bytes=VMEM_LIMIT_BYTES)


def _resident(shape):
    zeros = (0,) * len(shape)
    return pl.BlockSpec(shape, lambda *_: zeros, pipeline_mode=pl.Buffered(1))


def _rms(x, g):
    ms = jnp.mean(x * x, axis=-1, keepdims=True)
    return x * lax.rsqrt(ms + RMS_EPS) * g


def _dot(a, b):
    return jnp.dot(a, b, preferred_element_type=F32)


def _dot_nt(a, b):
    return lax.dot_general(a, b, (((1,), (1,)), ((), ())), preferred_element_type=F32)


def _ffn(h, g_ref, g_row, wi_ref, wo_ref):
    d_ff = wo_ref.shape[0]
    xn = _rms(h, g_ref[g_row:g_row + 1, :]).astype(BF16)
    acc = jnp.zeros(h.shape, F32)
    for c0 in range(0, d_ff, FFN_CHUNK):
        gate = _dot(xn, wi_ref[:, c0:c0 + FFN_CHUNK])
        up = _dot(xn, wi_ref[:, d_ff + c0:d_ff + c0 + FFN_CHUNK])
        act = (gate * jax.nn.sigmoid(gate)) * up
        acc = acc + _dot(act.astype(BF16), wo_ref[c0:c0 + FFN_CHUNK, :])
    return h + 0.5 * _rms(acc, g_ref[g_row + 1:g_row + 2, :])


def _deinterleave(ref, dil, row0, nrows):
    n = nrows // dil
    if ref.ndim == 2:
        return jnp.concatenate([ref[pl.ds(row0 + r, n, stride=dil), :] for r in range(dil)], axis=0)
    return jnp.concatenate(
        [jnp.concatenate([ref[c, pl.ds(row0 + r, n, stride=dil), :] for r in range(dil)], axis=0)
         for c in range(ref.shape[0])], axis=1)


def _ffn_proj_kernel(*refs, groups, n_rope, n_scaled, scale, use_rope, need_scratch):
    it = iter(refs)
    h_ref, g_ref, wi_ref, wo_ref, w_ref = [next(it) for _ in range(5)]
    tab_refs = [next(it) for _ in range(3)] if use_rope else []
    hout_ref = next(it)
    out_refs = [next(it) for _ in groups]
    xn_s = next(it) if need_scratch else None

    sub = h_ref.shape[0] // SUB_TILES
    for st in range(SUB_TILES):
        row0 = st * sub
        rows = slice(row0, row0 + sub)
        h1 = _ffn(h_ref[rows, :], g_ref, 0, wi_ref, wo_ref)
        hout_ref[rows, :] = h1
        xn = _rms(h1, g_ref[2:3, :])
        if need_scratch:
            for c in range(xn_s.shape[0]):
                xn_s[c, rows, :] = xn[:, c * LANES:(c + 1) * LANES]
        for (dil, col0, ncols), o_ref in zip(groups, out_refs):
            n = sub // dil
            m0 = row0 // dil
            xp = (xn if dil == 1 else _deinterleave(xn_s, dil, row0, sub)).astype(BF16)
            tabs = [t[rows, :] if dil == 1 else _deinterleave(t, dil, row0, sub) for t in tab_refs]
            for c0 in range(0, ncols, FFN_CHUNK):
                y = _dot(xp, w_ref[:, col0 + c0:col0 + c0 + FFN_CHUNK])
                for j0 in range(0, FFN_CHUNK, LANES):
                    yc = y[:, j0:j0 + LANES]
                    if c0 + j0 < n_rope:
                        cf, sa, sb = tabs
                        yc = (yc * cf + pltpu.roll(yc, LANES - ROT_DIM // 2, 1) * sa
                              + pltpu.roll(yc, ROT_DIM // 2, 1) * sb)
                    if c0 + j0 < n_scaled:
                        yc = yc * scale
                    yc = yc.astype(BF16)
                    for r in range(dil):
                        o_ref[r, m0:m0 + n, c0 + j0:c0 + j0 + LANES] = yc[r * n:(r + 1) * n]


def _ffn_proj_call(h, g8, wi, wo, w, tables, *, groups, n_rope, n_scaled, scale, tm):
    rows, d = h.shape
    tile = lambda width: pl.BlockSpec((tm, width), lambda i: (i, 0))
    use_rope = n_rope > 0
    need_scratch = any(dil > 1 for dil, _, _ in groups)
    in_specs = [tile(d), _resident(g8.shape), _resident(wi.shape), _resident(wo.shape), _resident(w.shape)]
    args = [h, g8, wi, wo, w]
    if use_rope:
        seq_tiles = tables[0].shape[0] // tm
        in_specs += [pl.BlockSpec((tm, LANES), lambda i: (i % seq_tiles, 0))] * 3
        args += list(tables)
    out_shape = [jax.ShapeDtypeStruct((rows, d), F32)]
    out_specs = [tile(d)]
    for dil, _, ncols in groups:
        out_shape.append(jax.ShapeDtypeStruct((dil, rows // dil, ncols), BF16))
        out_specs.append(pl.BlockSpec((dil, tm // dil, ncols), lambda i: (0, i, 0)))
    outs = pl.pallas_call(
        functools.partial(_ffn_proj_kernel, groups=tuple(groups), n_rope=n_rope, n_scaled=n_scaled, scale=scale,
                          use_rope=use_rope, need_scratch=need_scratch),
        grid=(rows // tm,),
        in_specs=in_specs,
        out_specs=out_specs,
        out_shape=out_shape,
        scratch_shapes=[pltpu.VMEM((d // LANES, tm, LANES), F32)] if need_scratch else [],
        compiler_params=_cparams(1),
        name="ffn_proj",
    )(*args)
    return outs[0], [o.reshape(rows, o.shape[2]) for o in outs[1:]]


def _swap_halves(x):
    half = LANES // 2
    return jnp.concatenate([x[:, half:], x[:, :half]], axis=1)


def _band_attn_kernel(*refs, radius, ch, tq, hq, group, chunks_per_seq, has_sink, want_lse):
    it = iter(refs)
    q_ref, k_ref, kp_ref, kn_ref, v_ref, vp_ref, vn_ref = [next(it) for _ in range(7)]
    sink_ref = next(it) if has_sink else None
    o_ref = next(it)
    lse_ref = next(it) if want_lse else None
    kbuf, vbuf, vtbuf = next(it), next(it), next(it)

    w = tq + 2 * radius
    nsb = ch // tq
    half = D_HEAD
    kbuf[0:radius, :] = kp_ref[...]
    kbuf[radius:radius + ch, :] = k_ref[...]
    kbuf[radius + ch:radius + ch + radius, :] = kn_ref[...]
    vbuf[0:radius, :] = vp_ref[...]
    vbuf[radius:radius + ch, :] = v_ref[...]
    vbuf[radius + ch:radius + ch + radius, :] = vn_ref[...]
    for i in range(vtbuf.shape[0]):
        for c in range(vbuf.shape[1] // LANES):
            blk = vbuf[i * LANES:(i + 1) * LANES, c * LANES:(c + 1) * LANES].astype(F32)
            vtbuf[i, c * LANES:(c + 1) * LANES, :] = blk.T.astype(BF16)

    cidx = pl.program_id(0) % chunks_per_seq
    seq_first = cidx == 0
    seq_last = cidx == chunks_per_seq - 1

    ki = lax.broadcasted_iota(jnp.int32, (w, tq), 0)
    qi = lax.broadcasted_iota(jnp.int32, (w, tq), 1)
    rel = ki - qi
    band_bias = jnp.where((rel >= 0) & (rel <= 2 * radius), 0.0, NEG_INF).astype(F32)
    head_halo = jnp.where(ki < radius, NEG_INF, 0.0).astype(F32)
    tail_halo = jnp.where(ki >= radius + tq, NEG_INF, 0.0).astype(F32)
    lane = lax.broadcasted_iota(jnp.int32, (tq, LANES), 1)
    m_lo = jnp.where(lane < half, 1.0, 0.0).astype(BF16)
    m_hi = jnp.where(lane < half, 0.0, 1.0).astype(BF16)
    lo_row = lax.broadcasted_iota(jnp.int32, (LANES, tq), 0) < half
    c_log2 = D_HEAD ** -0.5 * LOG2E
    nh = 2 if group == 1 else group
    n_blk = hq // nh

    def sub_block(sb):
        qs = sb * tq
        no_head = jnp.where(jnp.logical_and(seq_first, sb == 0), 1.0, 0.0).astype(F32)
        no_tail = jnp.where(jnp.logical_and(seq_last, sb == nsb - 1), 1.0, 0.0).astype(F32)
        bias1 = band_bias + head_halo * no_head + tail_halo * no_tail
        bias = jnp.concatenate([bias1] * nh, axis=1)
        for b in range(n_blk):
            if group == 1:
                kv_blk, kv_half = b, None
                q2 = q_ref[pl.ds(qs, tq), b * LANES:(b + 1) * LANES]
                parts = [q2 * m_lo, q2 * m_hi]
            else:
                kv_blk, kv_half = b // 2, b % 2
                parts = []
                for t in range(group // 2):
                    pair = b * (group // 2) + t
                    q2 = q_ref[pl.ds(qs, tq), pair * LANES:(pair + 1) * LANES]
                    if kv_half == 0:
                        parts += [q2 * m_lo, _swap_halves(q2) * m_lo]
                    else:
                        parts += [_swap_halves(q2) * m_hi, q2 * m_hi]
            k2 = kbuf[pl.ds(qs, w), kv_blk * LANES:(kv_blk + 1) * LANES]
            s = _dot_nt(k2, jnp.concatenate(parts, axis=0)) * c_log2 + bias
            m = jnp.max(s, axis=0, keepdims=True)
            if has_sink:
                sks = [sink_ref[b * nh + hh] * LOG2E for hh in range(nh)]
                m = jnp.concatenate([jnp.maximum(m[:, hh * tq:(hh + 1) * tq], sks[hh]) for hh in range(nh)], axis=1)
            pr = jnp.exp2(s - m)
            l = jnp.sum(pr, axis=0, keepdims=True)
            if has_sink:
                l = l + jnp.concatenate([jnp.exp2(sks[hh] - m[:, hh * tq:(hh + 1) * tq]) for hh in range(nh)],
                                        axis=1)
            pt = pr.astype(BF16)
            if kv_half is None:
                vrows = slice(kv_blk * LANES, (kv_blk + 1) * LANES)
            else:
                vrows = slice(kv_blk * LANES + kv_half * half, kv_blk * LANES + (kv_half + 1) * half)
            vt = jnp.concatenate([vtbuf[sb + u, vrows, :] for u in range(w // LANES)], axis=1)
            ot = _dot(vt, pt)
            for t in range(nh // 2):
                pair = b * (nh // 2) + t
                ca, cb = slice(2 * t * tq, (2 * t + 1) * tq), slice((2 * t + 1) * tq, (2 * t + 2) * tq)
                if kv_half is None:
                    acc = jnp.where(lo_row, ot[:, ca], ot[:, cb])
                else:
                    acc = jnp.concatenate([ot[:, ca], ot[:, cb]], axis=0)
                den = jnp.where(lo_row, l[:, ca], l[:, cb])
                o_ref[pl.ds(qs, tq), pair * LANES:(pair + 1) * LANES] = (acc / den).T.astype(o_ref.dtype)
                if want_lse:
                    lse = m * LN2 + jnp.log(l)
                    lse_ref[pl.ds(qs, tq), pair * LANES:(pair + 1) * LANES] = jnp.where(
                        lo_row, lse[:, ca], lse[:, cb]).T

    for sb in range(nsb):
        sub_block(sb)


def _band_attn_rows_kernel(*refs, radius, ch, tq, hq, group, chunks_per_seq, has_sink, want_lse):
    assert group == 1 and not has_sink
    it = iter(refs)
    q_ref, k_ref, kp_ref, kn_ref, v_ref, vp_ref, vn_ref = [next(it) for _ in range(7)]
    o_ref = next(it)
    lse_ref = next(it) if want_lse else None
    kbuf, vbuf = next(it), next(it)

    w = tq + 2 * radius
    nsb = ch // tq
    kbuf[0:radius, :] = kp_ref[...]
    kbuf[radius:radius + ch, :] = k_ref[...]
    kbuf[radius + ch:radius + ch + radius, :] = kn_ref[...]
    vbuf[0:radius, :] = vp_ref[...]
    vbuf[radius:radius + ch, :] = v_ref[...]
    vbuf[radius + ch:radius + ch + radius, :] = vn_ref[...]

    cidx = pl.program_id(0) % chunks_per_seq
    first_f = jnp.where(cidx == 0, 1.0, 0.0).astype(F32)
    last_f = jnp.where(cidx == chunks_per_seq - 1, 1.0, 0.0).astype(F32)

    qi = lax.broadcasted_iota(jnp.int32, (tq, w), 0)
    ki = lax.broadcasted_iota(jnp.int32, (tq, w), 1)
    rel = ki - qi
    band_bias = jnp.where((rel >= 0) & (rel <= 2 * radius), 0.0, NEG_INF).astype(F32)
    krow = lax.broadcasted_iota(jnp.int32, (1, w), 1)
    head_halo = jnp.where(krow < radius, NEG_INF, 0.0).astype(F32)
    tail_halo = jnp.where(krow >= radius + tq, NEG_INF, 0.0).astype(F32)
    lane = lax.broadcasted_iota(jnp.int32, (tq, LANES), 1)
    lo_lane = lane < D_HEAD
    m_lo = jnp.where(lo_lane, 1.0, 0.0).astype(BF16)
    m_hi = jnp.where(lo_lane, 0.0, 1.0).astype(BF16)
    c_log2 = D_HEAD ** -0.5 * LOG2E

    def sub_block(sb):
        qs = sb * tq
        bias = band_bias
        if sb == 0:
            bias = bias + head_halo * first_f
        if sb == nsb - 1:
            bias = bias + tail_halo * last_f
        for b in range(hq // 2):
            cols = slice(b * LANES, (b + 1) * LANES)
            q2 = q_ref[qs:qs + tq, cols]
            k2 = kbuf[qs:qs + w, cols]
            v2 = vbuf[qs:qs + w, cols]
            s = _dot_nt(jnp.concatenate([q2 * m_lo, q2 * m_hi], axis=0), k2)
            s = (s.reshape(2, tq, w) * c_log2 + bias[None]).reshape(2 * tq, w)
            m = jnp.max(s, axis=-1, keepdims=True)
            pr = jnp.exp2(s - m)
            l = jnp.sum(pr, axis=-1, keepdims=True)
            pv = _dot(pr.astype(BF16), v2)
            acc = jnp.where(lo_lane, pv[:tq], pv[tq:])
            den = jnp.where(lo_lane, l[:tq], l[tq:])
            o_ref[qs:qs + tq, cols] = (acc / den).astype(o_ref.dtype)
            if want_lse:
                lse = (m + jnp.log2(l)) * LN2
                lse_ref[qs:qs + tq, cols] = jnp.where(lo_lane, lse[:tq], lse[tq:])

    for sb in range(nsb):
        sub_block(sb)


def _band_attn_call(qkv, sink, *, seq_len, radius, hq, hkv, q_blk, k_blk, v_blk, want_lse, out_dtype,
                    transposed_scores):
    rows = qkv.shape[0]
    wq, wkv = hq * D_HEAD, hkv * D_HEAD
    ch = min(ATTN_CHUNK, seq_len)
    tq = min(ATTN_QBLOCK, ch)
    nchunks = rows // ch
    nhalo = rows // radius
    hpc = ch // radius

    def halo_prev(i):
        return jnp.maximum(i * hpc - 1, 0)

    def halo_next(i):
        return jnp.minimum((i + 1) * hpc, nhalo - 1)

    in_specs = [
        pl.BlockSpec((ch, wq), lambda i: (i, q_blk)),
        pl.BlockSpec((ch, wkv), lambda i: (i, k_blk)),
        pl.BlockSpec((radius, wkv), lambda i: (halo_prev(i), k_blk)),
        pl.BlockSpec((radius, wkv), lambda i: (halo_next(i), k_blk)),
        pl.BlockSpec((ch, wkv), lambda i: (i, v_blk)),
        pl.BlockSpec((radius, wkv), lambda i: (halo_prev(i), v_blk)),
        pl.BlockSpec((radius, wkv), lambda i: (halo_next(i), v_blk)),
    ]
    args = [qkv] * 7
    if sink is not None:
        in_specs.append(pl.BlockSpec(memory_space=pltpu.SMEM))
        args.append(sink)
    out_shape = [jax.ShapeDtypeStruct((rows, wq), out_dtype)]
    out_specs = [pl.BlockSpec((ch, wq), lambda i: (i, 0))]
    if want_lse:
        out_shape.append(jax.ShapeDtypeStruct((rows, wq), F32))
        out_specs.append(pl.BlockSpec((ch, wq), lambda i: (i, 0)))
    return pl.pallas_call(
        functools.partial(_band_attn_kernel if transposed_scores else _band_attn_rows_kernel,
                          radius=radius, ch=ch, tq=tq, hq=hq, group=hq // hkv,
                          chunks_per_seq=seq_len // ch, has_sink=sink is not None, want_lse=want_lse),
        grid=(nchunks,),
        in_specs=in_specs,
        out_specs=out_specs,
        out_shape=out_shape,
        scratch_shapes=[pltpu.VMEM((ch + 2 * radius, wkv), BF16), pltpu.VMEM((ch + 2 * radius, wkv), BF16)]
        + ([pltpu.VMEM(((ch + 2 * radius) // LANES, wkv, LANES), BF16)] if transposed_scores else []),
        compiler_params=_cparams(1),
        name="band_attn",
    )(*args)


def _na_bias_tables(rpb, grid_rows):
    heads = rpb.shape[0]
    nblk = grid_rows // NA_QROWS
    qc = np.arange(GRID_W)
    kc = np.arange(GRID_W)
    cs = np.clip(qc - NA_COLS // 2, 0, GRID_W - NA_COLS)
    col_ok = (kc[None, :] >= cs[:, None]) & (kc[None, :] < cs[:, None] + NA_COLS)
    ic = np.clip(kc[None, :] - qc[:, None] + NA_COLS - 1, 0, 2 * NA_COLS - 2)
    col_tiles = jnp.where(col_ok[None, None], rpb.astype(F32)[:, :, ic] * LOG2E, NEG_INF)
    masked = jnp.full((heads, 1, GRID_W, GRID_W), NEG_INF, F32)
    col_tiles = jnp.concatenate([col_tiles, masked], axis=1)
    variants = []
    for blk in (0, 1, nblk - 1):
        qr = blk * NA_QROWS + np.arange(NA_QROWS)
        kb = np.arange(3 * NA_QROWS) // NA_QROWS
        kr = (blk - 1) * NA_QROWS + np.arange(3 * NA_QROWS)
        exists = ~(((kb == 0) & (blk == 0)) | ((kb == 2) & (blk == nblk - 1)))
        rs = np.clip(qr - NA_ROWS // 2, 0, grid_rows - NA_ROWS)
        row_ok = exists[None, :] & (kr[None, :] >= rs[:, None]) & (kr[None, :] < rs[:, None] + NA_ROWS)
        ir = np.where(row_ok, kr[None, :] - qr[:, None] + NA_ROWS - 1, 2 * NA_ROWS - 1)
        t = col_tiles[:, ir]
        t = t.transpose(0, 1, 3, 2, 4).reshape(heads, NA_QROWS * GRID_W, 3 * NA_QROWS * GRID_W)
        variants.append(t)
    return jnp.stack(variants, axis=0)


def _na_kernel(q_ref, kp_ref, k_ref, kn_ref, vp_ref, v_ref, vn_ref, bias_ref, o_ref, kbuf, vbuf, *, n_pairs):
    nq = q_ref.shape[0]
    kbuf[0:nq, :] = kp_ref[...]
    kbuf[nq:2 * nq, :] = k_ref[...]
    kbuf[2 * nq:3 * nq, :] = kn_ref[...]
    vbuf[0:nq, :] = vp_ref[...]
    vbuf[nq:2 * nq, :] = v_ref[...]
    vbuf[2 * nq:3 * nq, :] = vn_ref[...]
    lane = lax.broadcasted_iota(jnp.int32, (nq, LANES), 1)
    lo_lane = lane < D_HEAD
    m_lo = jnp.where(lo_lane, 1.0, 0.0).astype(BF16)
    m_hi = jnp.where(lo_lane, 0.0, 1.0).astype(BF16)
    for p in range(n_pairs):
        cols = slice(p * LANES, (p + 1) * LANES)
        q2 = q_ref[:, cols]
        s = _dot_nt(jnp.concatenate([q2 * m_lo, q2 * m_hi], axis=0), kbuf[:, cols])
        s = s * LOG2E + jnp.concatenate([bias_ref[2 * p], bias_ref[2 * p + 1]], axis=0)
        m = jnp.max(s, axis=-1, keepdims=True)
        pr = jnp.exp2(s - m)
        l = jnp.sum(pr, axis=-1, keepdims=True)
        pv = _dot(pr.astype(BF16), vbuf[:, cols])
        acc = jnp.where(lo_lane, pv[:nq], pv[nq:])
        den = jnp.where(lo_lane, l[:nq], l[nq:])
        o_ref[:, cols] = (acc / den).astype(o_ref.dtype)


def _na_call(qkv, bias, *, batch, seq, heads):
    rows = qkv.shape[0]
    wq = heads * D_HEAD
    nq = NA_QROWS * GRID_W
    nblk = seq // nq

    def clamp_blk(b, j):
        return b * nblk + jnp.clip(j, 0, nblk - 1)

    def variant(j):
        return jnp.where(j == 0, 0, jnp.where(j == nblk - 1, 2, 1))

    blk = (nq, wq)
    in_specs = [
        pl.BlockSpec(blk, lambda b, j: (b * nblk + j, 0)),
        pl.BlockSpec(blk, lambda b, j: (clamp_blk(b, j - 1), 1)),
        pl.BlockSpec(blk, lambda b, j: (b * nblk + j, 1)),
        pl.BlockSpec(blk, lambda b, j: (clamp_blk(b, j + 1), 1)),
        pl.BlockSpec(blk, lambda b, j: (clamp_blk(b, j - 1), 2)),
        pl.BlockSpec(blk, lambda b, j: (b * nblk + j, 2)),
        pl.BlockSpec(blk, lambda b, j: (clamp_blk(b, j + 1), 2)),
        pl.BlockSpec((None, heads, nq, 3 * nq), lambda b, j: (variant(j), 0, 0, 0)),
    ]
    return pl.pallas_call(
        functools.partial(_na_kernel, n_pairs=heads // 2),
        grid=(batch, nblk),
        in_specs=in_specs,
        out_specs=pl.BlockSpec(blk, lambda b, j: (b * nblk + j, 0)),
        out_shape=jax.ShapeDtypeStruct((rows, wq), BF16),
        scratch_shapes=[pltpu.VMEM((3 * nq, wq), BF16), pltpu.VMEM((3 * nq, wq), BF16)],
        compiler_params=_cparams(2),
        name="na_attn",
    )(*([qkv] * 7), bias)


def _interleave(src_ref, dst_ref, dil, row0, nrows):
    if dil == 1:
        return src_ref[0, row0:row0 + nrows, :]
    n = nrows // dil
    m0 = row0 // dil
    for r in range(dil):
        blk = src_ref[r, m0:m0 + n, :]
        for c in range(dst_ref.shape[0]):
            dst_ref[c, pl.ds(row0 + r, n, stride=dil), :] = blk[:, c * LANES:(c + 1) * LANES]
    return jnp.concatenate([dst_ref[c, row0:row0 + nrows, :] for c in range(dst_ref.shape[0])], axis=1)


def _mix_ffn_ple_kernel(*refs, dils):
    ng = len(dils)
    it = iter(refs)
    o_refs = [next(it) for _ in range(ng)]
    l_refs = [next(it) for _ in range(ng)] if ng > 1 else []
    h_ref, p_ref, g_ref, wmix_ref, wi_ref, wo_ref, proj_ref, gate_ref, out_ref = [next(it) for _ in range(9)]
    scratch = {gi: (next(it), next(it)) for gi, dil in enumerate(dils) if ng > 1 and dil > 1}

    sub = h_ref.shape[0] // SUB_TILES
    for st in range(SUB_TILES):
        row0 = st * sub
        rows = slice(row0, row0 + sub)
        if ng > 1:
            os_, lses = [], []
            for gi, dil in enumerate(dils):
                o_s, l_s = scratch.get(gi, (None, None))
                os_.append(_interleave(o_refs[gi], o_s, dil, row0, sub))
                lses.append(_interleave(l_refs[gi], l_s, dil, row0, sub))
            mx = lses[0]
            for l in lses[1:]:
                mx = jnp.maximum(mx, l)
            es = [jnp.exp(l - mx) for l in lses]
            den = es[0]
            for e in es[1:]:
                den = den + e
            o = (es[0] / den) * os_[0]
            for e, og in zip(es[1:], os_[1:]):
                o = o + (e / den) * og
            o = o.astype(BF16)
        else:
            o = o_refs[0][rows, :]
        h = h_ref[rows, :] + _rms(_dot(o, wmix_ref[...]), g_ref[3:4, :])
        h = _ffn(h, g_ref, 4, wi_ref, wo_ref)
        e = _dot(p_ref[rows, :].astype(BF16), proj_ref[...])
        gate = jax.nn.sigmoid(_dot(_rms(h, g_ref[6:7, :]).astype(BF16), gate_ref[...]))
        out_ref[rows, :] = h + _rms(e * gate, g_ref[7:8, :])


def _mix_ffn_ple_call(os_, lses, dils, h, p, layer, g8, wmix, wi, wo, proj, gate, *, tm):
    rows, d = h.shape
    p_tile0 = layer * (rows // tm)
    tile = lambda width: pl.BlockSpec((tm, width), lambda i: (i, 0))
    args, in_specs, scratch = [], [], []
    if len(os_) > 1:
        for part in (os_, lses):
            for arr, dil in zip(part, dils):
                wq = arr.shape[1]
                args.append(arr.reshape(dil, rows // dil, wq))
                in_specs.append(pl.BlockSpec((dil, tm // dil, wq), lambda i: (0, i, 0)))
        for arr, dil in zip(os_, dils):
            if dil > 1:
                scratch += [pltpu.VMEM((arr.shape[1] // LANES, tm, LANES), F32)] * 2
    else:
        args.append(os_[0])
        in_specs.append(tile(os_[0].shape[1]))
    args += [h, p, g8, wmix, wi, wo, proj, gate]
    in_specs += [tile(d), pl.BlockSpec((tm, p.shape[1]), lambda i: (p_tile0 + i, 0)),
                 _resident(g8.shape), _resident(wmix.shape), _resident(wi.shape),
                 _resident(wo.shape), _resident(proj.shape), _resident(gate.shape)]
    return pl.pallas_call(
        functools.partial(_mix_ffn_ple_kernel, dils=tuple(dils)),
        grid=(rows // tm,),
        in_specs=in_specs,
        out_specs=tile(d),
        out_shape=jax.ShapeDtypeStruct(h.shape, F32),
        scratch_shapes=scratch,
        compiler_params=_cparams(1),
        name="mix_ffn_ple",
    )(*args)


def _rope_lane_tables(seq):
    half = ROT_DIM // 2
    pos = jnp.arange(seq, dtype=F32)
    inv = ROPE_THETA ** (-jnp.arange(0, ROT_DIM, 2, dtype=F32) / ROT_DIM)
    ang = pos[:, None] * inv[None, :]
    c, s = jnp.cos(ang), jnp.sin(ang)
    ones = jnp.ones((seq, D_HEAD - ROT_DIM), F32)
    z = lambda n: jnp.zeros((seq, n), F32)
    cf = jnp.concatenate([c, c, ones], axis=1)
    sa = jnp.concatenate([-s, z(D_HEAD - half)], axis=1)
    sb = jnp.concatenate([z(half), s, z(D_HEAD - ROT_DIM)], axis=1)
    return [jnp.concatenate([t, t], axis=1) for t in (cf, sa, sb)]


def kernel(x, p, norm_g, ffn_wi, ffn_wo, ple_proj, ple_gate, a_wqkv, a_wo, b_wqkv, b_wo, b_sink, c_wqkv, c_wo, c_rpb):
    batch, seq, d = x.shape
    depth = norm_g.shape[0]
    rows = batch * seq
    tm = min(TOKEN_TILE, seq)
    tables = _rope_lane_tables(seq)
    ng = len(A_PAIRS)
    wa = A_HEADS * D_HEAD
    bf = lambda a: a.astype(BF16)

    h = x.reshape(rows, d)
    for i in range(depth):
        g8 = norm_g[i]
        mixer, j = i % N_MIXERS, i // N_MIXERS
        ffn1 = (bf(ffn_wi[i, 0]), bf(ffn_wo[i, 0]))
        if mixer == 0:
            w = bf(a_wqkv[j]).reshape(d, 3, ng, wa).transpose(0, 2, 1, 3).reshape(d, ng * 3 * wa)
            groups = [(dil, gi * 3 * wa, 3 * wa) for gi, (_, dil) in enumerate(A_PAIRS)]
            h, qkvs = _ffn_proj_call(h, g8, *ffn1, w, tables, groups=groups, n_rope=2 * wa, n_scaled=0, scale=1.0,
                                     tm=tm)
            os_, lses = [], []
            for qkv, (window, dil) in zip(qkvs, A_PAIRS):
                o, lse = _band_attn_call(qkv, None, seq_len=seq // dil, radius=window // (2 * dil), hq=A_HEADS,
                                         hkv=A_HEADS, q_blk=0, k_blk=1, v_blk=2, want_lse=True, out_dtype=F32,
                                         transposed_scores=False)
                os_.append(o)
                lses.append(lse)
            dils = [dil for _, dil in A_PAIRS]
            wmix = a_wo[j]
        elif mixer == 1:
            wq, wkv = B_HEADS * D_HEAD, B_KV_HEADS * D_HEAD
            h, (qkv,) = _ffn_proj_call(h, g8, *ffn1, bf(b_wqkv[j]), tables, groups=[(1, 0, wq + 2 * wkv)],
                                       n_rope=wq + wkv, n_scaled=0, scale=1.0, tm=tm)
            os_ = _band_attn_call(qkv, b_sink[j].astype(F32), seq_len=seq, radius=B_RADIUS, hq=B_HEADS,
                                  hkv=B_KV_HEADS, q_blk=0, k_blk=wq // wkv, v_blk=wq // wkv + 1, want_lse=False,
                                  out_dtype=BF16, transposed_scores=True)
            lses, dils, wmix = [], [1], b_wo[j]
        else:
            wq = C_HEADS * D_HEAD
            h, (qkv,) = _ffn_proj_call(h, g8, *ffn1, bf(c_wqkv[j]), None, groups=[(1, 0, 3 * wq)], n_rope=0,
                                       n_scaled=wq, scale=D_HEAD ** -0.5, tm=tm)
            bias = _na_bias_tables(c_rpb[j], seq // GRID_W)
            os_ = [_na_call(qkv, bias, batch=batch, seq=seq, heads=C_HEADS)]
            lses, dils, wmix = [], [1], c_wo[j]
        h = _mix_ffn_ple_call(os_, lses, dils, h, p.reshape(depth * rows, -1), i, g8, bf(wmix), bf(ffn_wi[i, 1]),
                              bf(ffn_wo[i, 1]), bf(ple_proj[i]), bf(ple_gate[i]), tm=tm)
    return h.reshape(batch, seq, d)
```
